```python
import jax, jax.numpy as jnp
from jax import lax
import numpy as np

D_MODEL = 1024
BATCH = 8
SEQ = 4096
DEPTH = 2

N_HEADS = 16
HEAD_DIM = D_MODEL // N_HEADS
Q_BLOCK = 128
POOL_WINDOWS = (2, 4, 8, 16)
N_POOL_GROUPS = len(POOL_WINDOWS)
POOL_GROUP = D_MODEL // N_POOL_GROUPS
D_FF = 2816
N_MOD = 9
N_MIXERS = 2
N_ATTN_LAYERS = (DEPTH + 1) // 2
N_POOL_LAYERS = DEPTH // 2
EPS = 1e-6

kernel_name = "hybrid_stickbreak_pool_macaron_adaln"


def rmsnorm(x, g):
    xf = x.astype(jnp.float32)
    y = xf * lax.rsqrt(jnp.mean(xf * xf, axis=-1, keepdims=True) + EPS)
    return (y * g.astype(jnp.float32)).astype(x.dtype)


def modulate(h, shift, scale):
    return h * (1.0 + scale[:, None, :]) + shift[:, None, :]


def swiglu(h, w1, w2):
    gate, up = jnp.split(h @ w1, 2, axis=-1)
    return (jax.nn.silu(gate) * up) @ w2


def stick_breaking_attention(h, w_in, w_out):
    b, s_len, d = h.shape
    qkv = (h @ w_in).reshape(b, s_len, 3, N_HEADS, HEAD_DIM)
    q, k, v = qkv[:, :, 0], qkv[:, :, 1], qkv[:, :, 2]
    inv_sqrt_d = 1.0 / float(np.sqrt(HEAD_DIM))
    outs = []
    for qb in range(s_len // Q_BLOCK):
        t0 = qb * Q_BLOCK
        end = t0 + Q_BLOCK
        qi = q[:, t0:end]
        kj = k[:, :end]
        vj = v[:, :end]
        z = jnp.einsum('bqhd,bkhd->bhqk', qi, kj).astype(jnp.float32) * inv_sqrt_d
        t_pos = t0 + jnp.arange(Q_BLOCK)[:, None]
        s_pos = jnp.arange(end)[None, :]
        causal = s_pos < t_pos
        log_beta = jax.nn.log_sigmoid(z)
        log_one_minus = jnp.where(causal, -jax.nn.softplus(z), 0.0)
        shifted = jnp.concatenate([log_one_minus[..., 1:], jnp.zeros_like(log_one_minus[..., :1])], axis=-1)
        suffix = lax.cumsum(shifted, axis=3, reverse=True)
        a = jnp.where(causal, jnp.exp(log_beta + suffix), 0.0)
        outs.append(jnp.einsum('bhqk,bkhd->bqhd', a.astype(v.dtype), vj))
    o = jnp.concatenate(outs, axis=1).reshape(b, s_len, d)
    return o @ w_out


def multiscale_pool_mixer(h, w_in, w_group, scale, w_out):
    b, s_len, d = h.shape
    u = h @ w_in
    uf = u.astype(jnp.float32)
    cs = jnp.concatenate([jnp.zeros((b, 1, d), jnp.float32), jnp.cumsum(uf, axis=1)], axis=1)
    hi = cs[:, 1:]
    pos = jnp.arange(s_len, dtype=jnp.float32)[None, :, None]
    groups = []
    for gi, w in enumerate(POOL_WINDOWS):
        sl = slice(gi * POOL_GROUP, (gi + 1) * POOL_GROUP)
        lo = jnp.pad(cs[:, :, sl], ((0, 0), (w - 1, 0), (0, 0)))[:, :s_len]
        count = jnp.minimum(pos + 1.0, float(w))
        mean = (hi[:, :, sl] - lo) / count
        groups.append(mean - uf[:, :, sl])
    p = jnp.stack(groups, axis=2).astype(h.dtype)
    p = jnp.einsum('bsgc,gce->bsge', p, w_group).reshape(b, s_len, d)
    return (p * scale) @ w_out


def setup_inputs(seed: int = 0) -> dict:
    key = jax.random.key(seed)
    ks = jax.random.split(key, 16)

    def dense(k, shape, fan_in, gain=1.0):
        return jax.random.normal(k, shape, jnp.float32) * (gain * fan_in ** -0.5)

    x = jax.random.normal(ks[0], (BATCH, SEQ, D_MODEL), jnp.float32)
    c = jax.random.normal(ks[1], (BATCH, D_MODEL), jnp.float32)
    mod_w = dense(ks[2], (DEPTH, D_MODEL, N_MOD * D_MODEL), D_MODEL, 0.1)
    mod_b = 0.01 * jax.random.normal(ks[3], (DEPTH, N_MOD * D_MODEL), jnp.float32)
    norm_g = 1.0 + 0.05 * jax.random.normal(ks[4], (DEPTH, 3, D_MODEL), jnp.float32)
    ffn_w1 = dense(ks[5], (DEPTH, 2, D_MODEL, 2 * D_FF), D_MODEL)
    ffn_w2 = dense(ks[6], (DEPTH, 2, D_FF, D_MODEL), D_FF)
    attn_w_in = dense(ks[7], (N_ATTN_LAYERS, D_MODEL, 3 * D_MODEL), D_MODEL)
    attn_w_out = dense(ks[8], (N_ATTN_LAYERS, D_MODEL, D_MODEL), D_MODEL)
    pool_w_in = dense(ks[9], (N_POOL_LAYERS, D_MODEL, D_MODEL), D_MODEL)
    pool_w_group = dense(ks[10], (N_POOL_LAYERS, N_POOL_GROUPS, POOL_GROUP, POOL_GROUP), POOL_GROUP)
    pool_scale = 1.0 + 0.1 * jax.random.normal(ks[11], (N_POOL_LAYERS, D_MODEL), jnp.float32)
    pool_w_out = dense(ks[12], (N_POOL_LAYERS, D_MODEL, D_MODEL), D_MODEL)
    final_norm = 1.0 + 0.05 * jax.random.normal(ks[13], (D_MODEL,), jnp.float32)
    return {"x": x, "c": c, "mod_w": mod_w, "mod_b": mod_b, "norm_g": norm_g,
            "ffn_w1": ffn_w1, "ffn_w2": ffn_w2,
            "attn_w_in": attn_w_in, "attn_w_out": attn_w_out,
            "pool_w_in": pool_w_in, "pool_w_group": pool_w_group,
            "pool_scale": pool_scale, "pool_w_out": pool_w_out,
            "final_norm": final_norm}


def reference(x, c, mod_w, mod_b, norm_g, ffn_w1, ffn_w2, attn_w_in, attn_w_out,
              pool_w_in, pool_w_group, pool_scale, pool_w_out, final_norm):
    b = x.shape[0]
    c_act = jax.nn.silu(c)
    for i in range(DEPTH):
        mod = (c_act @ mod_w[i] + mod_b[i]).reshape(b, N_MOD, D_MODEL)
        sh1, sc1, g1, sh2, sc2, g2, sh3, sc3, g3 = [mod[:, j] for j in range(N_MOD)]
        h = modulate(rmsnorm(x, norm_g[i, 0]), sh1, sc1)
        x = x + 0.5 * (1.0 + g1)[:, None, :] * swiglu(h, ffn_w1[i, 0], ffn_w2[i, 0])
        h = modulate(rmsnorm(x, norm_g[i, 1]), sh2, sc2)
        j = i // N_MIXERS
        if i % N_MIXERS == 0:
            m = stick_breaking_attention(h, attn_w_in[j], attn_w_out[j])
        else:
            m = multiscale_pool_mixer(h, pool_w_in[j], pool_w_group[j], pool_scale[j], pool_w_out[j])
        x = x + (1.0 + g2)[:, None, :] * m
        h = modulate(rmsnorm(x, norm_g[i, 2]), sh3, sc3)
        x = x + 0.5 * (1.0 + g3)[:, None, :] * swiglu(h, ffn_w1[i, 1], ffn_w2[i, 1])
    return rmsnorm(x, final_norm)
```

```python
import functools

import jax
import jax.numpy as jnp
from jax import lax
from jax.experimental import pallas as pl
from jax.experimental.pallas import tpu as pltpu

D_MODEL = 1024
N_HEADS = 16
HEAD_DIM = D_MODEL // N_HEADS
Q_BLOCK = 128
POOL_WINDOWS = (2, 4, 8, 16)
POOL_GROUP = D_MODEL // len(POOL_WINDOWS)
D_FF = 2816
N_MOD = 9
EPS = 1e-6

LANES = 128
HEADS_PER_BLOCK = LANES // HEAD_DIM
POOL_HALO = 16
FFN_CHUNK = 256
ROW_TILE = 512
VMEM_LIMIT_BYTES = 56 * 1024 * 1024
EXP_UNDERFLOW = -104.0

F32 = jnp.float32
BF16 = jnp.bfloat16


def _params(semantics):
    return pltpu.CompilerParams(dimension_semantics=semantics, vmem_limit_bytes=VMEM_LIMIT_BYTES)


def _resident(shape):
    return pl.BlockSpec(shape, lambda *_: (0,) * len(shape), pipeline_mode=pl.Buffered(1))


def _norm_modulate(x, g, shift, scale):
    r = lax.rsqrt(jnp.mean(x * x, axis=-1, keepdims=True) + EPS)
    return ((x * r) * g) * (1.0 + scale) + shift


def _mod_kernel(c_ref, w_ref, b_ref, o_ref):
    c = c_ref[...]
    ca = (c * jax.nn.sigmoid(c)).astype(BF16)
    o_ref[...] = jnp.dot(ca, w_ref[...].astype(BF16), preferred_element_type=F32) + b_ref[...]


def _mod_table(c, mod_w, mod_b):
    depth, d, n = mod_w.shape
    b = c.shape[0]
    tn = 1024
    return pl.pallas_call(
        _mod_kernel,
        out_shape=jax.ShapeDtypeStruct((depth, b, n), F32),
        grid=(depth, n // tn),
        in_specs=[
            pl.BlockSpec((b, d), lambda i, j: (0, 0)),
            pl.BlockSpec((None, d, tn), lambda i, j: (i, 0, j)),
            pl.BlockSpec((None, 1, tn), lambda i, j: (i, 0, j)),
        ],
        out_specs=pl.BlockSpec((None, b, tn), lambda i, j: (i, 0, j)),
        compiler_params=_params(("parallel", "parallel")),
        name="mod_table",
    )(c, mod_w, mod_b.reshape(depth, 1, n))


def _ffn_kernel(x_ref, mod_ref, g_ref, w1_ref, w2_ref, fg_ref, o_ref, act_ref, *, mod_row, final_norm):
    x = x_ref[...]
    h = _norm_modulate(x, g_ref[...], mod_ref[mod_row:mod_row + 1, :], mod_ref[mod_row + 1:mod_row + 2, :])
    hb = h.astype(BF16)
    for c in range(D_FF // FFN_CHUNK):
        lo = c * FFN_CHUNK
        gate = jnp.dot(hb, w1_ref[:, lo:lo + FFN_CHUNK], preferred_element_type=F32)
        up = jnp.dot(hb, w1_ref[:, D_FF + lo:D_FF + lo + FFN_CHUNK], preferred_element_type=F32)
        act_ref[:, lo:lo + FFN_CHUNK] = ((gate * jax.nn.sigmoid(gate)) * up).astype(BF16)
    out = jnp.dot(act_ref[...], w2_ref[...], preferred_element_type=F32)
    y = x + (0.5 * (1.0 + mod_ref[mod_row + 2:mod_row + 3, :])) * out
    if final_norm:
        y = (y * lax.rsqrt(jnp.mean(y * y, axis=-1, keepdims=True) + EPS)) * fg_ref[...]
    o_ref[...] = y


def _ffn(x, mod, g, w1, w2, final_g, *, seq, mod_row, final_norm):
    t, d = x.shape
    tm = ROW_TILE
    per_batch = seq // tm
    return pl.pallas_call(
        functools.partial(_ffn_kernel, mod_row=mod_row, final_norm=final_norm),
        out_shape=jax.ShapeDtypeStruct((t, d), F32),
        grid=(t // tm,),
        in_specs=[
            pl.BlockSpec((tm, d), lambda i: (i, 0)),
            pl.BlockSpec((None, N_MOD, d), lambda i: (i // per_batch, 0, 0)),
            _resident((1, d)),
            _resident(w1.shape),
            _resident(w2.shape),
            _resident((1, d)),
        ],
        out_specs=pl.BlockSpec((tm, d), lambda i: (i, 0)),
        scratch_shapes=[pltpu.VMEM((tm, D_FF), BF16)],
        compiler_params=_params(("parallel",)),
        name="ffn",
    )(x, mod, g.reshape(1, d), w1, w2, final_g.reshape(1, d))


def _qkv_kernel(x_ref, mod_ref, g_ref, w_ref, o_ref):
    h = _norm_modulate(x_ref[...], g_ref[...], mod_ref[3:4, :], mod_ref[4:5, :])
    hb = h.astype(BF16)
    for c in range(3):
        lo = c * D_MODEL
        o_ref[:, lo:lo + D_MODEL] = jnp.dot(
            hb, w_ref[:, lo:lo + D_MODEL], preferred_element_type=F32).astype(o_ref.dtype)


def _qkv(x, mod, g, w_in, *, seq):
    t, d = x.shape
    tm = ROW_TILE
    per_batch = seq // tm
    return pl.pallas_call(
        _qkv_kernel,
        out_shape=jax.ShapeDtypeStruct((t, 3 * d), BF16),
        grid=(t // tm,),
        in_specs=[
            pl.BlockSpec((tm, d), lambda i: (i, 0)),
            pl.BlockSpec((None, N_MOD, d), lambda i: (i // per_batch, 0, 0)),
            _resident((1, d)),
            _resident(w_in.shape),
        ],
        out_specs=pl.BlockSpec((tm, 3 * d), lambda i: (i, 0)),
        compiler_params=_params(("parallel",)),
        name="qkv_proj",
    )(x, mod, g.reshape(1, d), w_in)


def _attn_kernel(q_ref, k_ref, v_ref, u_ref, o_ref, acc_ref, carry_ref):
    qi = pl.program_id(2)
    lane = lax.broadcasted_iota(jnp.int32, (Q_BLOCK, LANES), 1)
    qf = q_ref[...].astype(F32) * (1.0 / float(HEAD_DIM) ** 0.5)
    q2 = jnp.concatenate(
        [jnp.where(lane < HEAD_DIM, qf, 0.0), jnp.where(lane >= HEAD_DIM, qf, 0.0)], axis=0).astype(BF16)
    u = u_ref[...]
    row = lax.broadcasted_iota(jnp.int32, (HEADS_PER_BLOCK * Q_BLOCK, Q_BLOCK), 0) & (Q_BLOCK - 1)
    col = lax.broadcasted_iota(jnp.int32, (HEADS_PER_BLOCK * Q_BLOCK, Q_BLOCK), 1)
    causal = col < row

    def tile(j, carry, masked):
        start = pl.multiple_of(j * Q_BLOCK, Q_BLOCK)
        k_t = k_ref[pl.ds(start, Q_BLOCK), :]
        v_t = v_ref[pl.ds(start, Q_BLOCK), :]
        s = lax.dot_general(q2, k_t, (((1,), (1,)), ((), ())), preferred_element_type=F32)
        t = jnp.log(1.0 + jnp.exp(-jnp.abs(s)))
        m0 = jnp.minimum(s, 0.0)
        log_beta = m0 - t
        log_om = (m0 - s) - t
        if masked:
            log_om = jnp.where(causal, log_om, 0.0)
        hi = log_om.astype(BF16)
        lo = (log_om - hi.astype(F32)).astype(BF16)
        r = jnp.dot(hi, u, preferred_element_type=F32) + jnp.dot(lo, u, preferred_element_type=F32)
        a = jnp.exp(log_beta + r[:, :Q_BLOCK] + carry)
        if masked:
            a = jnp.where(causal, a, 0.0)
        pv = jnp.dot(a.astype(BF16), v_t, preferred_element_type=F32)
        return carry + r[:, Q_BLOCK:], pv

    c0, pv0 = tile(qi, jnp.zeros((HEADS_PER_BLOCK * Q_BLOCK, Q_BLOCK), F32), True)
    acc_ref[...] = pv0
    carry_ref[...] = c0

    def body(state):
        j, _ = state
        c, pv = tile(j, carry_ref[...], False)
        acc_ref[...] += pv
        carry_ref[...] = c
        return j - 1, jnp.max(c)

    lax.while_loop(lambda st: jnp.logical_and(st[0] >= 0, st[1] >= EXP_UNDERFLOW), body, (qi - 1, jnp.max(c0)))
    acc = acc_ref[...]
    o_ref[...] = jnp.where(lane < HEAD_DIM, acc[:Q_BLOCK], acc[Q_BLOCK:]).astype(o_ref.dtype)


def _suffix_matrix():
    j = lax.broadcasted_iota(jnp.int32, (Q_BLOCK, 2 * Q_BLOCK), 0)
    s = lax.broadcasted_iota(jnp.int32, (Q_BLOCK, 2 * Q_BLOCK), 1)
    return jnp.where(jnp.logical_or(s >= Q_BLOCK, j > s), 1.0, 0.0).astype(BF16)


def _attention(qkv, *, batch, seq):
    qkv3 = qkv.reshape(batch, seq, 3 * D_MODEL)
    pairs = D_MODEL // LANES
    return pl.pallas_call(
        _attn_kernel,
        out_shape=jax.ShapeDtypeStruct((batch, seq, D_MODEL), BF16),
        grid=(batch, pairs, seq // Q_BLOCK),
        in_specs=[
            pl.BlockSpec((None, Q_BLOCK, LANES), lambda b, p, i: (b, i, p)),
            pl.BlockSpec((None, seq, LANES), lambda b, p, i: (b, 0, pairs + p)),
            pl.BlockSpec((None, seq, LANES), lambda b, p, i: (b, 0, 2 * pairs + p)),
            pl.BlockSpec((Q_BLOCK, 2 * Q_BLOCK), lambda b, p, i: (0, 0)),
        ],
        out_specs=pl.BlockSpec((None, Q_BLOCK, LANES), lambda b, p, i: (b, i, p)),
        scratch_shapes=[pltpu.VMEM((HEADS_PER_BLOCK * Q_BLOCK, LANES), F32),
                        pltpu.VMEM((HEADS_PER_BLOCK * Q_BLOCK, Q_BLOCK), F32)],
        compiler_params=_params(("parallel", "parallel", "arbitrary")),
        name="stickbreak_attn",
    )(qkv3, qkv3, qkv3, _suffix_matrix()).reshape(batch * seq, D_MODEL)


def _attn_out_kernel(x_ref, o_ref, mod_ref, w_ref, y_ref):
    m = jnp.dot(o_ref[...], w_ref[...], preferred_element_type=F32)
    y_ref[...] = x_ref[...] + (1.0 + mod_ref[5:6, :]) * m


def _attn_out(x, o, mod, w_out, *, seq):
    t, d = x.shape
    tm = ROW_TILE
    per_batch = seq // tm
    return pl.pallas_call(
        _attn_out_kernel,
        out_shape=jax.ShapeDtypeStruct((t, d), F32),
        grid=(t // tm,),
        in_specs=[
            pl.BlockSpec((tm, d), lambda i: (i, 0)),
            pl.BlockSpec((tm, d), lambda i: (i, 0)),
            pl.BlockSpec((None, N_MOD, d), lambda i: (i // per_batch, 0, 0)),
            _resident(w_out.shape),
        ],
        out_specs=pl.BlockSpec((tm, d), lambda i: (i, 0)),
        compiler_params=_params(("parallel",)),
        name="attn_out",
    )(x, o, mod, w_out)


def _pool_kernel(x_ref, mod_ref, g_ref, win_ref, wg_ref, sc_ref, wout_ref, y_ref, u_ref, *, per_batch):
    i = pl.program_id(0)
    tm = x_ref.shape[0]
    x = x_ref[...]
    h = _norm_modulate(x, g_ref[...], mod_ref[3:4, :], mod_ref[4:5, :])
    u = jnp.dot(h.astype(BF16), win_ref[...], preferred_element_type=F32)
    first = (i % per_batch) == 0

    @pl.when(first)
    def _():
        u_ref[0:POOL_HALO, :] = jnp.zeros((POOL_HALO, D_MODEL), F32)

    @pl.when(jnp.logical_not(first))
    def _():
        u_ref[0:POOL_HALO, :] = u_ref[tm:tm + POOL_HALO, :]

    u_ref[POOL_HALO:POOL_HALO + tm, :] = u
    pos = (i % per_batch) * tm + lax.broadcasted_iota(jnp.int32, (tm, 1), 0)
    parts = []
    for gi, w in enumerate(POOL_WINDOWS):
        lo = gi * POOL_GROUP
        ug = u[:, lo:lo + POOL_GROUP]
        wsum = ug
        for dlt in range(1, w):
            wsum = wsum + u_ref[POOL_HALO - dlt:POOL_HALO - dlt + tm, lo:lo + POOL_GROUP]
        count = jnp.minimum(pos + 1, w).astype(F32)
        p = (wsum / count - ug).astype(BF16)
        parts.append(jnp.dot(p, wg_ref[gi], preferred_element_type=F32))
    pg = jnp.concatenate(parts, axis=1) * sc_ref[...]
    m = jnp.dot(pg.astype(BF16), wout_ref[...], preferred_element_type=F32)
    y_ref[...] = x + (1.0 + mod_ref[5:6, :]) * m


def _pool(x, mod, g, w_in, w_group, scale, w_out, *, seq):
    t, d = x.shape
    tm = ROW_TILE
    per_batch = seq // tm
    return pl.pallas_call(
        functools.partial(_pool_kernel, per_batch=per_batch),
        out_shape=jax.ShapeDtypeStruct((t, d), F32),
        grid=(t // tm,),
        in_specs=[
            pl.BlockSpec((tm, d), lambda i: (i, 0)),
            pl.BlockSpec((None, N_MOD, d), lambda i: (i // per_batch, 0, 0)),
            _resident((1, d)),
            _resident(w_in.shape),
            _resident(w_group.shape),
            _resident((1, d)),
            _resident(w_out.shape),
        ],
        out_specs=pl.BlockSpec((tm, d), lambda i: (i, 0)),
        scratch_shapes=[pltpu.VMEM((tm + POOL_HALO, d), F32)],
        compiler_params=_params(("arbitrary",)),
        name="pool_mixer",
    )(x, mod, g.reshape(1, d), w_in, w_group, scale.reshape(1, d), w_out)


def kernel(x, c, mod_w, mod_b, norm_g, ffn_w1, ffn_w2, attn_w_in, attn_w_out,
           pool_w_in, pool_w_group, pool_scale, pool_w_out, final_norm):
    batch, seq, d = x.shape
    depth = mod_w.shape[0]
    assert d == D_MODEL and seq % ROW_TILE == 0 and seq % Q_BLOCK == 0
    mod = _mod_table(c, mod_w, mod_b).reshape(depth, batch, N_MOD, d)
    xf = x.reshape(batch * seq, d)
    for i in range(depth):
        last = i == depth - 1
        xf = _ffn(xf, mod[i], norm_g[i, 0], ffn_w1[i, 0].astype(BF16), ffn_w2[i, 0].astype(BF16), final_norm,
                  seq=seq, mod_row=0, final_norm=False)
        j = i // 2
        if i % 2 == 0:
            qkv = _qkv(xf, mod[i], norm_g[i, 1], attn_w_in[j].astype(BF16), seq=seq)
            o = _attention(qkv, batch=batch, seq=seq)
            xf = _attn_out(xf, o, mod[i], attn_w_out[j].astype(BF16), seq=seq)
        else:
            xf = _pool(xf, mod[i], norm_g[i, 1], pool_w_in[j].astype(BF16), pool_w_group[j].astype(BF16),
                       pool_scale[j], pool_w_out[j].astype(BF16), seq=seq)
        xf = _ffn(xf, mod[i], norm_g[i, 2], ffn_w1[i, 1].astype(BF16), ffn_w2[i, 1].astype(BF16), final_norm,
                  seq=seq, mod_row=6, final_norm=last)
    return xf.reshape(batch, seq, d)
```

```python
import functools

import jax
import jax.numpy as jnp
from jax import lax
from jax.experimental import pallas as pl
from jax.experimental.pallas import tpu as pltpu

D_MODEL = 1024
N_HEADS = 16
HEAD_DIM = D_MODEL // N_HEADS
Q_BLOCK = 128
POOL_WINDOWS = (2, 4, 8, 16)
POOL_GROUP = D_MODEL // len(POOL_WINDOWS)
D_FF = 2816
N_MOD = 9
EPS = 1e-6

LANES = 128
HEADS_PER_BLOCK = LANES // HEAD_DIM
Q_SUB = 64
ATTN_WINDOW = 256
ATTN_TAIL = LANES
Q_GROUP = 512
POOL_HALO = 16
FFN_CHUNK = 256
ROW_TILE = 512
VMEM_LIMIT_BYTES = 56 * 1024 * 1024
EXP_UNDERFLOW = -104.0

F32 = jnp.float32
BF16 = jnp.bfloat16


def _params(semantics):
    return pltpu.CompilerParams(dimension_semantics=semantics, vmem_limit_bytes=VMEM_LIMIT_BYTES)


def _resident(shape):
    return pl.BlockSpec(shape, lambda *_: (0,) * len(shape), pipeline_mode=pl.Buffered(1))


def _norm_modulate(x, g, shift, scale):
    r = lax.rsqrt(jnp.mean(x * x, axis=-1, keepdims=True) + EPS)
    return ((x * r) * g) * (1.0 + scale) + shift


def _mod_kernel(c_ref, w_ref, b_ref, o_ref):
    c = c_ref[...]
    ca = (c * jax.nn.sigmoid(c)).astype(BF16)
    o_ref[...] = jnp.dot(ca, w_ref[...].astype(BF16), preferred_element_type=F32) + b_ref[...]


def _mod_table(c, mod_w, mod_b):
    depth, d, n = mod_w.shape
    b = c.shape[0]
    tn = 1024
    return pl.pallas_call(
        _mod_kernel,
        out_shape=jax.ShapeDtypeStruct((depth, b, n), F32),
        grid=(depth, n // tn),
        in_specs=[
            pl.BlockSpec((b, d), lambda i, j: (0, 0)),
            pl.BlockSpec((None, d, tn), lambda i, j: (i, 0, j)),
            pl.BlockSpec((None, 1, tn), lambda i, j: (i, 0, j)),
        ],
        out_specs=pl.BlockSpec((None, b, tn), lambda i, j: (i, 0, j)),
        compiler_params=_params(("parallel", "parallel")),
        name="mod_table",
    )(c, mod_w, mod_b.reshape(depth, 1, n))


def _ffn_kernel(*refs, mod_row, final_norm, attn_mix):
    if attn_mix:
        x_ref, ao_ref, wo_ref, *refs = refs
    else:
        x_ref, *refs = refs
    mod_ref, g_ref, w1_ref, w2_ref, fg_ref, o_ref, act_ref = refs
    x = x_ref[...]
    if attn_mix:
        x = x + (1.0 + mod_ref[5:6, :]) * jnp.dot(ao_ref[...], wo_ref[...], preferred_element_type=F32)
    h = _norm_modulate(x, g_ref[...], mod_ref[mod_row:mod_row + 1, :], mod_ref[mod_row + 1:mod_row + 2, :])
    hb = h.astype(BF16)
    for c in range(D_FF // FFN_CHUNK):
        lo = c * FFN_CHUNK
        gate = jnp.dot(hb, w1_ref[:, lo:lo + FFN_CHUNK], preferred_element_type=F32)
        up = jnp.dot(hb, w1_ref[:, D_FF + lo:D_FF + lo + FFN_CHUNK], preferred_element_type=F32)
        act_ref[:, lo:lo + FFN_CHUNK] = ((gate * jax.nn.sigmoid(gate)) * up).astype(BF16)
    out = jnp.dot(act_ref[...], w2_ref[...], preferred_element_type=F32)
    y = x + (0.5 * (1.0 + mod_ref[mod_row + 2:mod_row + 3, :])) * out
    if final_norm:
        y = (y * lax.rsqrt(jnp.mean(y * y, axis=-1, keepdims=True) + EPS)) * fg_ref[...]
    o_ref[...] = y


def _ffn(x, mod, g, w1, w2, final_g, *, seq, mod_row, final_norm, attn=None):
    t, d = x.shape
    tm = ROW_TILE
    per_batch = seq // tm
    row_spec = pl.BlockSpec((tm, d), lambda i: (i, 0))
    mix_specs = [row_spec, _resident(attn[1].shape)] if attn else []
    return pl.pallas_call(
        functools.partial(_ffn_kernel, mod_row=mod_row, final_norm=final_norm, attn_mix=bool(attn)),
        out_shape=jax.ShapeDtypeStruct((t, d), F32),
        grid=(t // tm,),
        in_specs=[row_spec] + mix_specs + [
            pl.BlockSpec((None, N_MOD, d), lambda i: (i // per_batch, 0, 0)),
            _resident((1, d)),
            _resident(w1.shape),
            _resident(w2.shape),
            _resident((1, d)),
        ],
        out_specs=row_spec,
        scratch_shapes=[pltpu.VMEM((tm, D_FF), BF16)],
        compiler_params=_params(("parallel",)),
        name="attn_out_ffn" if attn else "ffn",
    )(x, *(attn or ()), mod, g.reshape(1, d), w1, w2, final_g.reshape(1, d))


def _qkv_kernel(x_ref, mod_ref, g_ref, w_ref, o_ref):
    h = _norm_modulate(x_ref[...], g_ref[...], mod_ref[3:4, :], mod_ref[4:5, :])
    hb = h.astype(BF16)
    for c in range(3):
        lo = c * D_MODEL
        o_ref[:, lo:lo + D_MODEL] = jnp.dot(
            hb, w_ref[:, lo:lo + D_MODEL], preferred_element_type=F32).astype(o_ref.dtype)


def _qkv(x, mod, g, w_in, *, seq):
    t, d = x.shape
    tm = ROW_TILE
    per_batch = seq // tm
    return pl.pallas_call(
        _qkv_kernel,
        out_shape=jax.ShapeDtypeStruct((t, 3 * d), BF16),
        grid=(t // tm,),
        in_specs=[
            pl.BlockSpec((tm, d), lambda i: (i, 0)),
            pl.BlockSpec((None, N_MOD, d), lambda i: (i // per_batch, 0, 0)),
            _resident((1, d)),
            _resident(w_in.shape),
        ],
        out_specs=pl.BlockSpec((tm, 3 * d), lambda i: (i, 0)),
        compiler_params=_params(("parallel",)),
        name="qkv_proj",
    )(x, mod, g.reshape(1, d), w_in)


def _log_terms(s):
    sign_bit = jnp.uint32(0x80000000)
    neg_abs = lax.bitcast_convert_type(lax.bitcast_convert_type(s, jnp.uint32) | sign_bit, F32)
    log_beta = jnp.minimum(s, 0.0) - jnp.log(1.0 + jnp.exp(neg_abs))
    return log_beta, log_beta - s


def _suffix_sums(log_om, u):
    hi = log_om.astype(BF16)
    lo = (log_om - hi.astype(F32)).astype(BF16)
    return jnp.dot(hi, u, preferred_element_type=F32) + jnp.dot(lo, u, preferred_element_type=F32)


def _attn_kernel(q_ref, k_ref, v_ref, u_ref, o_ref, acc_ref, carry_ref, *, seq):
    rows = HEADS_PER_BLOCK * Q_SUB
    lane = lax.broadcasted_iota(jnp.int32, (Q_SUB, LANES), 1)
    qrow = lax.broadcasted_iota(jnp.int32, (rows, ATTN_WINDOW), 0) & (Q_SUB - 1)
    wcol = lax.broadcasted_iota(jnp.int32, (rows, ATTN_WINDOW), 1)
    tcol = lax.broadcasted_iota(jnp.int32, (rows, ATTN_TAIL), 1)
    nt_dims = (((1,), (1,)), ((), ()))

    def stacked_q(t0):
        qf = q_ref[pl.ds(t0, Q_SUB), :].astype(F32) * (1.0 / float(HEAD_DIM) ** 0.5)
        return jnp.concatenate(
            [jnp.where(lane < HEAD_DIM, qf, 0.0), jnp.where(lane >= HEAD_DIM, qf, 0.0)], axis=0).astype(BF16)

    def windows(geo):
        q2s = [stacked_q(t0) for t0, _, _ in geo]
        scores = [lax.dot_general(q2, k_ref[pl.ds(start, ATTN_WINDOW), :], nt_dims, preferred_element_type=F32)
                  for q2, (_, start, _) in zip(q2s, geo)]
        causals = [wcol < qrow + lead for _, _, lead in geo]
        terms = [_log_terms(s) for s in scores]
        log_oms = [jnp.where(c, lom, 0.0) for c, (_, lom) in zip(causals, terms)]
        suffixes = [_suffix_sums(lom, u_ref[...]) for lom in log_oms]
        weights = [jnp.where(c, jnp.exp(lb + suf), 0.0).astype(BF16)
                   for c, (lb, _), suf in zip(causals, terms, suffixes)]
        pvs = [jnp.dot(a, v_ref[pl.ds(start, ATTN_WINDOW), :], preferred_element_type=F32)
               for a, (_, start, _) in zip(weights, geo)]
        carries = [jnp.sum(lom, axis=-1, keepdims=True) for lom in log_oms]
        return list(zip(q2s, pvs, carries))

    def walk_tail(q2, pv, carry, start):
        acc_ref[...] = pv
        carry_ref[...] = jnp.broadcast_to(carry, (rows, LANES))

        def body(state):
            hi_excl, _ = state
            lo_incl = pl.multiple_of(jnp.maximum(hi_excl - ATTN_TAIL, 0), Q_SUB)
            s = lax.dot_general(q2, k_ref[pl.ds(lo_incl, ATTN_TAIL), :], nt_dims, preferred_element_type=F32)
            log_beta, log_om = _log_terms(s)
            fresh = tcol + lo_incl < hi_excl
            log_om = jnp.where(fresh, log_om, 0.0)
            c = carry_ref[...]
            suffix = _suffix_sums(log_om, u_ref[0:ATTN_TAIL, 0:ATTN_TAIL])
            a = jnp.where(fresh, jnp.exp(log_beta + suffix + c), 0.0)
            acc_ref[...] += jnp.dot(a.astype(BF16), v_ref[pl.ds(lo_incl, ATTN_TAIL), :],
                                    preferred_element_type=F32)
            c = c + jnp.sum(log_om, axis=-1, keepdims=True)
            carry_ref[...] = c
            return lo_incl, jnp.max(c)

        lax.while_loop(lambda st: jnp.logical_and(st[0] > 0, st[1] >= EXP_UNDERFLOW), body,
                       (jnp.int32(start), jnp.max(carry)))
        return acc_ref[...]

    def store(t0, pv):
        o_ref[pl.ds(t0, Q_SUB), :] = jnp.where(lane < HEAD_DIM, pv[:Q_SUB], pv[Q_SUB:]).astype(o_ref.dtype)

    def group(base, static_base):
        geo = []
        for sb in range(Q_GROUP // Q_SUB):
            lead = ATTN_WINDOW - Q_SUB
            if static_base:
                t0 = base + sb * Q_SUB
                lead = min(lead, t0)
                start = t0 - lead
            else:
                t0 = pl.multiple_of(base + sb * Q_SUB, Q_SUB)
                start = pl.multiple_of(t0 - lead, Q_SUB)
            geo.append((t0, start, lead))
        blocks = [(t0, start) + res for (t0, start, _), res in zip(geo, windows(geo))]
        tails = [blk for blk in blocks if not (static_base and blk[1] == 0)]
        if not tails:
            for t0, _, _, pv, _ in blocks:
                store(t0, pv)
            return
        worst = tails[0][4]
        for blk in tails[1:]:
            worst = jnp.maximum(worst, blk[4])
        need_tail = jnp.max(worst) >= EXP_UNDERFLOW

        @pl.when(jnp.logical_not(need_tail))
        def _():
            for t0, _, _, pv, _ in blocks:
                store(t0, pv)

        @pl.when(need_tail)
        def _():
            for blk in blocks:
                t0, start, q2, pv, carry = blk
                store(t0, walk_tail(q2, pv, carry, start) if any(blk is t for t in tails) else pv)

    group(0, True)

    def loop_body(g, _):
        group(g * Q_GROUP, False)
        return 0

    lax.fori_loop(1, seq // Q_GROUP, loop_body, 0)


def _suffix_matrix():
    j = lax.broadcasted_iota(jnp.int32, (ATTN_WINDOW, ATTN_WINDOW), 0)
    s = lax.broadcasted_iota(jnp.int32, (ATTN_WINDOW, ATTN_WINDOW), 1)
    return jnp.where(j > s, 1.0, 0.0).astype(BF16)


def _attention(qkv, *, batch, seq):
    assert seq % Q_GROUP == 0 and seq >= ATTN_WINDOW
    qkv3 = qkv.reshape(batch, seq, 3 * D_MODEL)
    pairs = D_MODEL // LANES
    rows = HEADS_PER_BLOCK * Q_SUB
    return pl.pallas_call(
        functools.partial(_attn_kernel, seq=seq),
        out_shape=jax.ShapeDtypeStruct((batch, seq, D_MODEL), BF16),
        grid=(batch, pairs),
        in_specs=[
            pl.BlockSpec((None, seq, LANES), lambda b, p: (b, 0, p)),
            pl.BlockSpec((None, seq, LANES), lambda b, p: (b, 0, pairs + p)),
            pl.BlockSpec((None, seq, LANES), lambda b, p: (b, 0, 2 * pairs + p)),
            _resident((ATTN_WINDOW, ATTN_WINDOW)),
        ],
        out_specs=pl.BlockSpec((None, seq, LANES), lambda b, p: (b, 0, p)),
        scratch_shapes=[pltpu.VMEM((rows, LANES), F32), pltpu.VMEM((rows, LANES), F32)],
        compiler_params=_params(("parallel", "parallel")),
        name="stickbreak_attn",
    )(qkv3, qkv3, qkv3, _suffix_matrix()).reshape(batch * seq, D_MODEL)


def _pool_kernel(x_ref, mod_ref, g_ref, win_ref, wg_ref, sc_ref, wout_ref, y_ref, u_ref, *, per_batch):
    i = pl.program_id(0)
    tm = x_ref.shape[0]
    x = x_ref[...]
    h = _norm_modulate(x, g_ref[...], mod_ref[3:4, :], mod_ref[4:5, :])
    u = jnp.dot(h.astype(BF16), win_ref[...], preferred_element_type=F32)
    first = (i % per_batch) == 0

    @pl.when(first)
    def _():
        u_ref[0:POOL_HALO, :] = jnp.zeros((POOL_HALO, D_MODEL), F32)

    @pl.when(jnp.logical_not(first))
    def _():
        u_ref[0:POOL_HALO, :] = u_ref[tm:tm + POOL_HALO, :]

    u_ref[POOL_HALO:POOL_HALO + tm, :] = u
    pos = (i % per_batch) * tm + lax.broadcasted_iota(jnp.int32, (tm, 1), 0)
    parts = []
    for gi, w in enumerate(POOL_WINDOWS):
        lo = gi * POOL_GROUP
        wsum = u_ref[:, lo:lo + POOL_GROUP]
        d = 1
        while d < w:
            wsum = wsum + pltpu.roll(wsum, d, 0)
            d *= 2
        inv_count = 1.0 / jnp.minimum(pos + 1, w).astype(F32)
        p = (wsum[POOL_HALO:, :] * inv_count - u[:, lo:lo + POOL_GROUP]).astype(BF16)
        parts.append(jnp.dot(p, wg_ref[gi], preferred_element_type=F32))
    pg = jnp.concatenate(parts, axis=1) * sc_ref[...]
    m = jnp.dot(pg.astype(BF16), wout_ref[...], preferred_element_type=F32)
    y_ref[...] = x + (1.0 + mod_ref[5:6, :]) * m


def _pool(x, mod, g, w_in, w_group, scale, w_out, *, seq):
    t, d = x.shape
    tm = ROW_TILE
    per_batch = seq // tm
    assert POOL_HALO >= max(POOL_WINDOWS) - 1
    return pl.pallas_call(
        functools.partial(_pool_kernel, per_batch=per_batch),
        out_shape=jax.ShapeDtypeStruct((t, d), F32),
        grid=(t // tm,),
        in_specs=[
            pl.BlockSpec((tm, d), lambda i: (i, 0)),
            pl.BlockSpec((None, N_MOD, d), lambda i: (i // per_batch, 0, 0)),
            _resident((1, d)),
            _resident(w_in.shape),
            _resident(w_group.shape),
            _resident((1, d)),
            _resident(w_out.shape),
        ],
        out_specs=pl.BlockSpec((tm, d), lambda i: (i, 0)),
        scratch_shapes=[pltpu.VMEM((tm + POOL_HALO, d), F32)],
        compiler_params=_params(("arbitrary",)),
        name="pool_mixer",
    )(x, mod, g.reshape(1, d), w_in, w_group, scale.reshape(1, d), w_out)


def kernel(x, c, mod_w, mod_b, norm_g, ffn_w1, ffn_w2, attn_w_in, attn_w_out,
           pool_w_in, pool_w_group, pool_scale, pool_w_out, final_norm):
    batch, seq, d = x.shape
    depth = mod_w.shape[0]
    assert d == D_MODEL and seq % ROW_TILE == 0 and seq % Q_BLOCK == 0
    mod = _mod_table(c, mod_w, mod_b).reshape(depth, batch, N_MOD, d)
    xf = x.reshape(batch * seq, d)
    for i in range(depth):
        last = i == depth - 1
        xf = _ffn(xf, mod[i], norm_g[i, 0], ffn_w1[i, 0].astype(BF16), ffn_w2[i, 0].astype(BF16), final_norm,
                  seq=seq, mod_row=0, final_norm=False)
        j = i // 2
        attn = None
        if i % 2 == 0:
            qkv = _qkv(xf, mod[i], norm_g[i, 1], attn_w_in[j].astype(BF16), seq=seq)
            attn = (_attention(qkv, batch=batch, seq=seq), attn_w_out[j].astype(BF16))
        else:
            xf = _pool(xf, mod[i], norm_g[i, 1], pool_w_in[j].astype(BF16), pool_w_group[j].astype(BF16),
                       pool_scale[j], pool_w_out[j].astype(BF16), seq=seq)
        xf = _ffn(xf, mod[i], norm_g[i, 2], ffn_w1[i, 1].astype(BF16), ffn_w2[i, 1].astype(BF16), final_norm,
                  seq=seq, mod_row=6, final_norm=last, attn=attn)
    return xf.reshape(batch, seq, d)
```

```python
import functools

import jax
import jax.numpy as jnp
from jax import lax
from jax.experimental import pallas as pl
from jax.experimental.pallas import tpu as pltpu

D_MODEL = 1024
N_HEADS = 16
HEAD_DIM = D_MODEL // N_HEADS
Q_BLOCK = 128
POOL_WINDOWS = (2, 4, 8, 16)
POOL_GROUP = D_MODEL // len(POOL_WINDOWS)
D_FF = 2816
N_MOD = 9
EPS = 1e-6

LANES = 128
HEADS_PER_BLOCK = LANES // HEAD_DIM
Q_SUB = 64
ATTN_WINDOW = 256
ATTN_TAIL = LANES
Q_GROUP = 1024
POOL_HALO = 16
FFN_CHUNK = 256
ROW_TILE = 512
ROW_SPLIT = 2
VMEM_LIMIT_BYTES = 56 * 1024 * 1024
EXP_UNDERFLOW = -104.0

F32 = jnp.float32
BF16 = jnp.bfloat16


def _params(semantics):
    return pltpu.CompilerParams(dimension_semantics=semantics, vmem_limit_bytes=VMEM_LIMIT_BYTES)


def _resident(shape, lead=()):
    block = (None,) * len(lead) + tuple(shape[len(lead):])
    index = tuple(lead) + (0,) * (len(shape) - len(lead))
    return pl.BlockSpec(block, lambda *_: index, pipeline_mode=pl.Buffered(1))


def _row_parts(rows):
    part = rows // ROW_SPLIT
    return [slice(r * part, (r + 1) * part) for r in range(ROW_SPLIT)]


def _norm_modulate(x, g, shift, scale):
    r = lax.rsqrt(jnp.mean(x * x, axis=-1, keepdims=True) + EPS)
    return ((x * r) * g) * (1.0 + scale) + shift


def _mod_kernel(c_ref, w_ref, b_ref, o_ref):
    c = c_ref[...]
    ca = (c * jax.nn.sigmoid(c)).astype(BF16)
    o_ref[...] = jnp.dot(ca, w_ref[...].astype(BF16), preferred_element_type=F32) + b_ref[...]


def _mod_table(c, mod_w, mod_b):
    depth, d, n = mod_w.shape
    b = c.shape[0]
    tn = 1024
    return pl.pallas_call(
        _mod_kernel,
        out_shape=jax.ShapeDtypeStruct((depth, b, n), F32),
        grid=(depth, n // tn),
        in_specs=[
            pl.BlockSpec((b, d), lambda i, j: (0, 0)),
            pl.BlockSpec((None, d, tn), lambda i, j: (i, 0, j)),
            pl.BlockSpec((None, 1, tn), lambda i, j: (i, 0, j)),
        ],
        out_specs=pl.BlockSpec((None, b, tn), lambda i, j: (i, 0, j)),
        compiler_params=_params(("parallel", "parallel")),
        name="mod_table",
    )(c, mod_w, mod_b.reshape(depth, 1, n))


def _ffn_kernel(*refs, mod_row, final_norm, attn_mix):
    if attn_mix:
        x_ref, ao_ref, wo_ref, *refs = refs
    else:
        x_ref, *refs = refs
    mod_ref, g_ref, w1_ref, w2_ref, fg_ref, o_ref, act_ref = refs
    for rows in _row_parts(x_ref.shape[0]):
        x = x_ref[rows, :]
        if attn_mix:
            x = x + (1.0 + mod_ref[5:6, :]) * jnp.dot(ao_ref[rows, :], wo_ref[...], preferred_element_type=F32)
        h = _norm_modulate(x, g_ref[...], mod_ref[mod_row:mod_row + 1, :], mod_ref[mod_row + 1:mod_row + 2, :])
        hb = h.astype(BF16)
        for c in range(D_FF // FFN_CHUNK):
            lo = c * FFN_CHUNK
            gate = jnp.dot(hb, w1_ref[:, lo:lo + FFN_CHUNK], preferred_element_type=F32)
            up = jnp.dot(hb, w1_ref[:, D_FF + lo:D_FF + lo + FFN_CHUNK], preferred_element_type=F32)
            act_ref[rows, lo:lo + FFN_CHUNK] = ((gate * jax.nn.sigmoid(gate)) * up).astype(BF16)
        out = jnp.dot(act_ref[rows, :], w2_ref[...], preferred_element_type=F32)
        y = x + (0.5 * (1.0 + mod_ref[mod_row + 2:mod_row + 3, :])) * out
        if final_norm:
            y = (y * lax.rsqrt(jnp.mean(y * y, axis=-1, keepdims=True) + EPS)) * fg_ref[...]
        o_ref[rows, :] = y


def _ffn(x, mod, g, w1, w2, widx, final_g, *, seq, mod_row, final_norm, attn=None):
    t, d = x.shape
    tm = ROW_TILE
    per_batch = seq // tm
    row_spec = pl.BlockSpec((tm, d), lambda i: (i, 0))
    mix_specs = [row_spec, _resident(attn[1].shape)] if attn else []
    return pl.pallas_call(
        functools.partial(_ffn_kernel, mod_row=mod_row, final_norm=final_norm, attn_mix=bool(attn)),
        out_shape=jax.ShapeDtypeStruct((t, d), F32),
        grid=(t // tm,),
        in_specs=[row_spec] + mix_specs + [
            pl.BlockSpec((None, N_MOD, d), lambda i: (i // per_batch, 0, 0)),
            _resident((1, d)),
            _resident(w1.shape, widx),
            _resident(w2.shape, widx),
            _resident((1, d)),
        ],
        out_specs=row_spec,
        scratch_shapes=[pltpu.VMEM((tm, D_FF), BF16)],
        compiler_params=_params(("parallel",)),
        name="attn_out_ffn" if attn else "ffn",
    )(x, *(attn or ()), mod, g.reshape(1, d), w1, w2, final_g.reshape(1, d))


def _qkv_kernel(x_ref, mod_ref, g_ref, w_ref, o_ref):
    for rows in _row_parts(x_ref.shape[0]):
        h = _norm_modulate(x_ref[rows, :], g_ref[...], mod_ref[3:4, :], mod_ref[4:5, :])
        hb = h.astype(BF16)
        for c in range(3):
            lo = c * D_MODEL
            o_ref[rows, lo:lo + D_MODEL] = jnp.dot(
                hb, w_ref[:, lo:lo + D_MODEL], preferred_element_type=F32).astype(o_ref.dtype)


def _qkv(x, mod, g, w_in, *, seq):
    t, d = x.shape
    tm = ROW_TILE
    per_batch = seq // tm
    return pl.pallas_call(
        _qkv_kernel,
        out_shape=jax.ShapeDtypeStruct((t, 3 * d), BF16),
        grid=(t // tm,),
        in_specs=[
            pl.BlockSpec((tm, d), lambda i: (i, 0)),
            pl.BlockSpec((None, N_MOD, d), lambda i: (i // per_batch, 0, 0)),
            _resident((1, d)),
            _resident(w_in.shape),
        ],
        out_specs=pl.BlockSpec((tm, 3 * d), lambda i: (i, 0)),
        compiler_params=_params(("parallel",)),
        name="qkv_proj",
    )(x, mod, g.reshape(1, d), w_in)


def _log_terms(s):
    log_beta = jnp.minimum(s, 0.0) - jnp.log(1.0 + jnp.exp(-jnp.abs(s)))
    return log_beta, log_beta - s


def _suffix_sums(log_om, u):
    hi = log_om.astype(BF16)
    lo = (log_om - hi.astype(F32)).astype(BF16)
    return jnp.dot(hi, u, preferred_element_type=F32) + jnp.dot(lo, u, preferred_element_type=F32)


def _attn_kernel(q_ref, k_ref, v_ref, u_ref, o_ref, acc_ref, carry_ref, *, seq):
    rows = HEADS_PER_BLOCK * Q_SUB
    lane = lax.broadcasted_iota(jnp.int32, (Q_SUB, LANES), 1)
    qrow = lax.broadcasted_iota(jnp.int32, (rows, ATTN_WINDOW), 0) & (Q_SUB - 1)
    wcol = lax.broadcasted_iota(jnp.int32, (rows, ATTN_WINDOW), 1)
    tcol = lax.broadcasted_iota(jnp.int32, (rows, ATTN_TAIL), 1)
    nt_dims = (((1,), (1,)), ((), ()))

    def stacked_q(t0):
        qf = q_ref[pl.ds(t0, Q_SUB), :].astype(F32) * (1.0 / float(HEAD_DIM) ** 0.5)
        return jnp.concatenate(
            [jnp.where(lane < HEAD_DIM, qf, 0.0), jnp.where(lane >= HEAD_DIM, qf, 0.0)], axis=0).astype(BF16)

    def windows(geo):
        q2s = [stacked_q(t0) for t0, _, _ in geo]
        scores = [lax.dot_general(q2, k_ref[pl.ds(start, ATTN_WINDOW), :], nt_dims, preferred_element_type=F32)
                  for q2, (_, start, _) in zip(q2s, geo)]
        causals = [wcol < qrow + lead for _, _, lead in geo]
        terms = [_log_terms(s) for s in scores]
        log_oms = [jnp.where(c, lom, 0.0) for c, (_, lom) in zip(causals, terms)]
        suffixes = [_suffix_sums(lom, u_ref[...]) for lom in log_oms]
        weights = [jnp.where(c, jnp.exp(lb + suf), 0.0).astype(BF16)
                   for c, (lb, _), suf in zip(causals, terms, suffixes)]
        pvs = [jnp.dot(a, v_ref[pl.ds(start, ATTN_WINDOW), :], preferred_element_type=F32)
               for a, (_, start, _) in zip(weights, geo)]
        carries = [jnp.sum(lom, axis=-1, keepdims=True) for lom in log_oms]
        return list(zip(q2s, pvs, carries))

    def walk_tail(q2, pv, carry, start):
        acc_ref[...] = pv
        carry_ref[...] = jnp.broadcast_to(carry, (rows, LANES))

        def body(state):
            hi_excl, _ = state
            lo_incl = pl.multiple_of(jnp.maximum(hi_excl - ATTN_TAIL, 0), Q_SUB)
            s = lax.dot_general(q2, k_ref[pl.ds(lo_incl, ATTN_TAIL), :], nt_dims, preferred_element_type=F32)
            log_beta, log_om = _log_terms(s)
            fresh = tcol + lo_incl < hi_excl
            log_om = jnp.where(fresh, log_om, 0.0)
            c = carry_ref[...]
            suffix = _suffix_sums(log_om, u_ref[0:ATTN_TAIL, 0:ATTN_TAIL])
            a = jnp.where(fresh, jnp.exp(log_beta + suffix + c), 0.0)
            acc_ref[...] += jnp.dot(a.astype(BF16), v_ref[pl.ds(lo_incl, ATTN_TAIL), :],
                                    preferred_element_type=F32)
            c = c + jnp.sum(log_om, axis=-1, keepdims=True)
            carry_ref[...] = c
            return lo_incl, jnp.max(c)

        lax.while_loop(lambda st: jnp.logical_and(st[0] > 0, st[1] >= EXP_UNDERFLOW), body,
                       (jnp.int32(start), jnp.max(carry)))
        return acc_ref[...]

    def store(t0, pv):
        o_ref[pl.ds(t0, Q_SUB), :] = jnp.where(lane < HEAD_DIM, pv[:Q_SUB], pv[Q_SUB:]).astype(o_ref.dtype)

    def group(base, static_base):
        geo = []
        for sb in range(Q_GROUP // Q_SUB):
            lead = ATTN_WINDOW - Q_SUB
            if static_base:
                t0 = base + sb * Q_SUB
                lead = min(lead, t0)
                start = t0 - lead
            else:
                t0 = pl.multiple_of(base + sb * Q_SUB, Q_SUB)
                start = pl.multiple_of(t0 - lead, Q_SUB)
            geo.append((t0, start, lead))
        blocks = [(t0, start) + res for (t0, start, _), res in zip(geo, windows(geo))]
        tails = [blk for blk in blocks if not (static_base and blk[1] == 0)]
        if not tails:
            for t0, _, _, pv, _ in blocks:
                store(t0, pv)
            return
        worst = tails[0][4]
        for blk in tails[1:]:
            worst = jnp.maximum(worst, blk[4])
        need_tail = jnp.max(worst) >= EXP_UNDERFLOW

        @pl.when(jnp.logical_not(need_tail))
        def _():
            for t0, _, _, pv, _ in blocks:
                store(t0, pv)

        @pl.when(need_tail)
        def _():
            for blk in blocks:
                t0, start, q2, pv, carry = blk
                store(t0, walk_tail(q2, pv, carry, start) if any(blk is t for t in tails) else pv)

    group(0, True)

    def loop_body(g, _):
        group(g * Q_GROUP, False)
        return 0

    lax.fori_loop(1, seq // Q_GROUP, loop_body, 0)


def _suffix_matrix():
    j = lax.broadcasted_iota(jnp.int32, (ATTN_WINDOW, ATTN_WINDOW), 0)
    s = lax.broadcasted_iota(jnp.int32, (ATTN_WINDOW, ATTN_WINDOW), 1)
    return jnp.where(j > s, 1.0, 0.0).astype(BF16)


def _attention(qkv, *, batch, seq):
    assert seq % Q_GROUP == 0 and seq >= ATTN_WINDOW
    qkv3 = qkv.reshape(batch, seq, 3 * D_MODEL)
    pairs = D_MODEL // LANES
    rows = HEADS_PER_BLOCK * Q_SUB
    return pl.pallas_call(
        functools.partial(_attn_kernel, seq=seq),
        out_shape=jax.ShapeDtypeStruct((batch, seq, D_MODEL), BF16),
        grid=(batch, pairs),
        in_specs=[
            pl.BlockSpec((None, seq, LANES), lambda b, p: (b, 0, p)),
            pl.BlockSpec((None, seq, LANES), lambda b, p: (b, 0, pairs + p)),
            pl.BlockSpec((None, seq, LANES), lambda b, p: (b, 0, 2 * pairs + p)),
            _resident((ATTN_WINDOW, ATTN_WINDOW)),
        ],
        out_specs=pl.BlockSpec((None, seq, LANES), lambda b, p: (b, 0, p)),
        scratch_shapes=[pltpu.VMEM((rows, LANES), F32), pltpu.VMEM((rows, LANES), F32)],
        compiler_params=_params(("parallel", "parallel")),
        name="stickbreak_attn",
    )(qkv3, qkv3, qkv3, _suffix_matrix()).reshape(batch * seq, D_MODEL)


def _pool_kernel(x_ref, mod_ref, g_ref, win_ref, wg_ref, sc_ref, wout_ref, y_ref, u_ref, *, per_batch):
    i = pl.program_id(0)
    tm = x_ref.shape[0]
    first = (i % per_batch) == 0

    @pl.when(first)
    def _():
        u_ref[0:POOL_HALO, :] = jnp.zeros((POOL_HALO, D_MODEL), F32)

    @pl.when(jnp.logical_not(first))
    def _():
        u_ref[0:POOL_HALO, :] = u_ref[tm:tm + POOL_HALO, :]

    row_parts = _row_parts(tm)
    for rows in row_parts:
        h = _norm_modulate(x_ref[rows, :], g_ref[...], mod_ref[3:4, :], mod_ref[4:5, :])
        u_ref[POOL_HALO + rows.start:POOL_HALO + rows.stop, :] = jnp.dot(
            h.astype(BF16), win_ref[...], preferred_element_type=F32)
    for rows in row_parts:
        n = rows.stop - rows.start
        pos = (i % per_batch) * tm + rows.start + lax.broadcasted_iota(jnp.int32, (n, 1), 0)
        groups = []
        for gi, w in enumerate(POOL_WINDOWS):
            lo = gi * POOL_GROUP
            wsum = u_ref[rows.start:rows.stop + POOL_HALO, lo:lo + POOL_GROUP]
            d = 1
            while d < w:
                wsum = wsum + pltpu.roll(wsum, d, 0)
                d *= 2
            inv_count = 1.0 / jnp.minimum(pos + 1, w).astype(F32)
            u = u_ref[POOL_HALO + rows.start:POOL_HALO + rows.stop, lo:lo + POOL_GROUP]
            p = (wsum[POOL_HALO:, :] * inv_count - u).astype(BF16)
            groups.append(jnp.dot(p, wg_ref[gi], preferred_element_type=F32))
        pg = jnp.concatenate(groups, axis=1) * sc_ref[...]
        m = jnp.dot(pg.astype(BF16), wout_ref[...], preferred_element_type=F32)
        y_ref[rows, :] = x_ref[rows, :] + (1.0 + mod_ref[5:6, :]) * m


def _pool(x, mod, g, w_in, w_group, scale, w_out, *, seq):
    t, d = x.shape
    tm = ROW_TILE
    per_batch = seq // tm
    assert POOL_HALO >= max(POOL_WINDOWS) - 1
    return pl.pallas_call(
        functools.partial(_pool_kernel, per_batch=per_batch),
        out_shape=jax.ShapeDtypeStruct((t, d), F32),
        grid=(t // tm,),
        in_specs=[
            pl.BlockSpec((tm, d), lambda i: (i, 0)),
            pl.BlockSpec((None, N_MOD, d), lambda i: (i // per_batch, 0, 0)),
            _resident((1, d)),
            _resident(w_in.shape),
            _resident(w_group.shape),
            _resident((1, d)),
            _resident(w_out.shape),
        ],
        out_specs=pl.BlockSpec((tm, d), lambda i: (i, 0)),
        scratch_shapes=[pltpu.VMEM((tm + POOL_HALO, d), F32)],
        compiler_params=_params(("arbitrary",)),
        name="pool_mixer",
    )(x, mod, g.reshape(1, d), w_in, w_group, scale.reshape(1, d), w_out)


def kernel(x, c, mod_w, mod_b, norm_g, ffn_w1, ffn_w2, attn_w_in, attn_w_out,
           pool_w_in, pool_w_group, pool_scale, pool_w_out, final_norm):
    batch, seq, d = x.shape
    depth = mod_w.shape[0]
    assert d == D_MODEL and seq % ROW_TILE == 0 and seq % Q_BLOCK == 0
    mod = _mod_table(c, mod_w, mod_b).reshape(depth, batch, N_MOD, d)
    xf = x.reshape(batch * seq, d)
    w1, w2 = ffn_w1.astype(BF16), ffn_w2.astype(BF16)
    for i in range(depth):
        last = i == depth - 1
        xf = _ffn(xf, mod[i], norm_g[i, 0], w1, w2, (i, 0), final_norm, seq=seq, mod_row=0, final_norm=False)
        j = i // 2
        attn = None
        if i % 2 == 0:
            qkv = _qkv(xf, mod[i], norm_g[i, 1], attn_w_in[j].astype(BF16), seq=seq)
            attn = (_attention(qkv, batch=batch, seq=seq), attn_w_out[j].astype(BF16))
        else:
            xf = _pool(xf, mod[i], norm_g[i, 1], pool_w_in[j].astype(BF16), pool_w_group[j].astype(BF16),
                       pool_scale[j], pool_w_out[j].astype(BF16), seq=seq)
        xf = _ffn(xf, mod[i], norm_g[i, 2], w1, w2, (i, 1), final_norm,
                  seq=seq, mod_row=6, final_norm=last, attn=attn)
    return xf.reshape(batch, seq, d)
```

```python
import functools

import jax
import jax.numpy as jnp
from jax import lax
from jax.experimental import pallas as pl
from jax.experimental.pallas import tpu as pltpu

D_MODEL = 1024
N_HEADS = 16
HEAD_DIM = D_MODEL // N_HEADS
Q_BLOCK = 128
POOL_WINDOWS = (2, 4, 8, 16)
POOL_GROUP = D_MODEL // len(POOL_WINDOWS)
D_FF = 2816
N_MOD = 9
EPS = 1e-6

LANES = 128
HEADS_PER_BLOCK = LANES // HEAD_DIM
Q_SUB = 64
ATTN_WINDOW = 256
ATTN_TAIL = LANES
Q_GROUP = 1024
ATTN_SKEW = 3
POOL_HALO = 16
FFN_CHUNK = 256
ROW_TILE = 1024
ROW_SPLIT = 2
VMEM_LIMIT_BYTES = 56 * 1024 * 1024
EXP_UNDERFLOW = -104.0
MASKED_SCORE = -1e30
LOG2E = 1.4426950408889634

F32 = jnp.float32
BF16 = jnp.bfloat16


def _params(semantics):
    return pltpu.CompilerParams(dimension_semantics=semantics, vmem_limit_bytes=VMEM_LIMIT_BYTES)


def _resident(shape, lead=()):
    block = (None,) * len(lead) + tuple(shape[len(lead):])
    index = tuple(lead) + (0,) * (len(shape) - len(lead))
    return pl.BlockSpec(block, lambda *_: index, pipeline_mode=pl.Buffered(1))


def _row_parts(rows):
    part = rows // ROW_SPLIT
    return [slice(r * part, (r + 1) * part) for r in range(ROW_SPLIT)]


def _norm_modulate(x, g, shift, scale):
    r = lax.rsqrt(jnp.mean(x * x, axis=-1, keepdims=True) + EPS)
    return ((x * r) * g) * (1.0 + scale) + shift


def _mod_kernel(c_ref, w_ref, b_ref, o_ref):
    c = c_ref[...]
    ca = (c * jax.nn.sigmoid(c)).astype(BF16)
    o_ref[...] = jnp.dot(ca, w_ref[...].astype(BF16), preferred_element_type=F32) + b_ref[...]


def _mod_table(c, mod_w, mod_b):
    depth, d, n = mod_w.shape
    b = c.shape[0]
    tn = 1024
    return pl.pallas_call(
        _mod_kernel,
        out_shape=jax.ShapeDtypeStruct((depth, b, n), F32),
        grid=(depth, n // tn),
        in_specs=[
            pl.BlockSpec((b, d), lambda i, j: (0, 0)),
            pl.BlockSpec((None, d, tn), lambda i, j: (i, 0, j)),
            pl.BlockSpec((None, 1, tn), lambda i, j: (i, 0, j)),
        ],
        out_specs=pl.BlockSpec((None, b, tn), lambda i, j: (i, 0, j)),
        compiler_params=_params(("parallel", "parallel")),
        name="mod_table",
    )(c, mod_w, mod_b.reshape(depth, 1, n))


def _ffn_kernel(*refs, mod_row, final_norm, attn_mix):
    if attn_mix:
        x_ref, ao_ref, wo_ref, *refs = refs
    else:
        x_ref, *refs = refs
    mod_ref, g_ref, w1_ref, w2_ref, fg_ref, o_ref, act_ref = refs
    for rows in _row_parts(x_ref.shape[0]):
        x = x_ref[rows, :]
        if attn_mix:
            x = x + (1.0 + mod_ref[5:6, :]) * jnp.dot(ao_ref[rows, :], wo_ref[...], preferred_element_type=F32)
        h = _norm_modulate(x, g_ref[...], mod_ref[mod_row:mod_row + 1, :], mod_ref[mod_row + 1:mod_row + 2, :])
        hb = h.astype(BF16)
        for c in range(D_FF // FFN_CHUNK):
            lo = c * FFN_CHUNK
            gate = jnp.dot(hb, w1_ref[:, lo:lo + FFN_CHUNK], preferred_element_type=F32)
            up = jnp.dot(hb, w1_ref[:, D_FF + lo:D_FF + lo + FFN_CHUNK], preferred_element_type=F32)
            act_ref[rows, lo:lo + FFN_CHUNK] = ((gate * jax.nn.sigmoid(gate)) * up).astype(BF16)
        out = jnp.dot(act_ref[rows, :], w2_ref[...], preferred_element_type=F32)
        y = x + (0.5 * (1.0 + mod_ref[mod_row + 2:mod_row + 3, :])) * out
        if final_norm:
            y = (y * lax.rsqrt(jnp.mean(y * y, axis=-1, keepdims=True) + EPS)) * fg_ref[...]
        o_ref[rows, :] = y


def _ffn(x, mod, g, w1, w2, widx, final_g, *, seq, mod_row, final_norm, attn=None):
    t, d = x.shape
    tm = ROW_TILE
    per_batch = seq // tm
    row_spec = pl.BlockSpec((tm, d), lambda i: (i, 0))
    mix_specs = [row_spec, _resident(attn[1].shape)] if attn else []
    return pl.pallas_call(
        functools.partial(_ffn_kernel, mod_row=mod_row, final_norm=final_norm, attn_mix=bool(attn)),
        out_shape=jax.ShapeDtypeStruct((t, d), F32),
        grid=(t // tm,),
        in_specs=[row_spec] + mix_specs + [
            pl.BlockSpec((None, N_MOD, d), lambda i: (i // per_batch, 0, 0)),
            _resident((1, d)),
            _resident(w1.shape, widx),
            _resident(w2.shape, widx),
            _resident((1, d)),
        ],
        out_specs=row_spec,
        scratch_shapes=[pltpu.VMEM((tm, D_FF), BF16)],
        compiler_params=_params(("parallel",)),
        name="attn_out_ffn" if attn else "ffn",
    )(x, *(attn or ()), mod, g.reshape(1, d), w1, w2, final_g.reshape(1, d))


def _qkv_kernel(x_ref, mod_ref, g_ref, w_ref, o_ref):
    for rows in _row_parts(x_ref.shape[0]):
        h = _norm_modulate(x_ref[rows, :], g_ref[...], mod_ref[3:4, :], mod_ref[4:5, :])
        hb = h.astype(BF16)
        for c in range(3):
            lo = c * D_MODEL
            o_ref[rows, lo:lo + D_MODEL] = jnp.dot(
                hb, w_ref[:, lo:lo + D_MODEL], preferred_element_type=F32).astype(o_ref.dtype)


def _qkv(x, mod, g, w_in, *, seq):
    t, d = x.shape
    tm = ROW_TILE
    per_batch = seq // tm
    return pl.pallas_call(
        _qkv_kernel,
        out_shape=jax.ShapeDtypeStruct((t, 3 * d), BF16),
        grid=(t // tm,),
        in_specs=[
            pl.BlockSpec((tm, d), lambda i: (i, 0)),
            pl.BlockSpec((None, N_MOD, d), lambda i: (i // per_batch, 0, 0)),
            _resident((1, d)),
            _resident(w_in.shape),
        ],
        out_specs=pl.BlockSpec((tm, 3 * d), lambda i: (i, 0)),
        compiler_params=_params(("parallel",)),
        name="qkv_proj",
    )(x, mod, g.reshape(1, d), w_in)


def _log_terms(s, valid):
    s = jnp.where(valid, s, MASKED_SCORE)
    log_beta = jnp.minimum(s, 0.0) - jnp.log(1.0 + jnp.exp2(jnp.abs(s) * -LOG2E))
    return log_beta, log_beta - s


def _suffix_sums(log_om, u):
    return jnp.dot(log_om.astype(BF16), u, preferred_element_type=F32)


def _attn_kernel(q_ref, k_ref, v_ref, u_ref, o_ref, acc_ref, carry_ref, *, seq):
    rows = HEADS_PER_BLOCK * Q_SUB
    lane = lax.broadcasted_iota(jnp.int32, (Q_SUB, LANES), 1)
    qrow = lax.broadcasted_iota(jnp.int32, (rows, ATTN_WINDOW), 0) & (Q_SUB - 1)
    wcol = lax.broadcasted_iota(jnp.int32, (rows, ATTN_WINDOW), 1)
    tcol = lax.broadcasted_iota(jnp.int32, (rows, ATTN_TAIL), 1)
    nt_dims = (((1,), (1,)), ((), ()))

    def stacked_q(t0):
        qf = q_ref[pl.ds(t0, Q_SUB), :].astype(F32) * (1.0 / float(HEAD_DIM) ** 0.5)
        return jnp.concatenate(
            [jnp.where(lane < HEAD_DIM, qf, 0.0), jnp.where(lane >= HEAD_DIM, qf, 0.0)], axis=0).astype(BF16)

    def windows(geo):
        n = len(geo)
        q2s, scores, log_betas, carries, suffixes, pvs = ([None] * n for _ in range(6))
        for step in range(n + 2 * ATTN_SKEW):
            if step < n:
                t0, start, _ = geo[step]
                q2s[step] = stacked_q(t0)
                scores[step] = lax.dot_general(q2s[step], k_ref[pl.ds(start, ATTN_WINDOW), :], nt_dims,
                                               preferred_element_type=F32)
            j = step - ATTN_SKEW
            if 0 <= j < n:
                log_betas[j], log_om = _log_terms(scores[j], wcol < qrow + geo[j][2])
                suffixes[j] = _suffix_sums(log_om, u_ref[...])
                carries[j] = suffixes[j][:, 0:1] + log_om[:, 0:1]
            j = step - 2 * ATTN_SKEW
            if 0 <= j < n:
                a = jnp.exp(log_betas[j] + suffixes[j]).astype(BF16)
                pvs[j] = jnp.dot(a, v_ref[pl.ds(geo[j][1], ATTN_WINDOW), :], preferred_element_type=F32)
        return list(zip(q2s, pvs, carries))

    def walk_tail(q2, pv, carry, start):
        acc_ref[...] = pv
        carry_ref[...] = jnp.broadcast_to(carry, (rows, LANES))

        def body(state):
            hi_excl, _ = state
            lo_incl = pl.multiple_of(jnp.maximum(hi_excl - ATTN_TAIL, 0), Q_SUB)
            s = lax.dot_general(q2, k_ref[pl.ds(lo_incl, ATTN_TAIL), :], nt_dims, preferred_element_type=F32)
            log_beta, log_om = _log_terms(s, tcol + lo_incl < hi_excl)
            c = carry_ref[...]
            suffix = _suffix_sums(log_om, u_ref[0:ATTN_TAIL, 0:ATTN_TAIL])
            a = jnp.exp(log_beta + suffix + c)
            acc_ref[...] += jnp.dot(a.astype(BF16), v_ref[pl.ds(lo_incl, ATTN_TAIL), :],
                                    preferred_element_type=F32)
            c = c + (suffix[:, 0:1] + log_om[:, 0:1])
            carry_ref[...] = c
            return lo_incl, jnp.max(c)

        lax.while_loop(lambda st: jnp.logical_and(st[0] > 0, st[1] >= EXP_UNDERFLOW), body,
                       (jnp.int32(start), jnp.max(carry)))
        return acc_ref[...]

    def store(t0, pv):
        o_ref[pl.ds(t0, Q_SUB), :] = jnp.where(lane < HEAD_DIM, pv[:Q_SUB], pv[Q_SUB:]).astype(o_ref.dtype)

    def group(base, static_base):
        geo = []
        for sb in range(Q_GROUP // Q_SUB):
            lead = ATTN_WINDOW - Q_SUB
            if static_base:
                t0 = base + sb * Q_SUB
                lead = min(lead, t0)
                start = t0 - lead
            else:
                t0 = pl.multiple_of(base + sb * Q_SUB, Q_SUB)
                start = pl.multiple_of(t0 - lead, Q_SUB)
            geo.append((t0, start, lead))
        blocks = [(t0, start) + res for (t0, start, _), res in zip(geo, windows(geo))]
        tails = [blk for blk in blocks if not (static_base and blk[1] == 0)]
        if not tails:
            for t0, _, _, pv, _ in blocks:
                store(t0, pv)
            return
        worst = tails[0][4]
        for blk in tails[1:]:
            worst = jnp.maximum(worst, blk[4])
        need_tail = jnp.max(worst) >= EXP_UNDERFLOW

        @pl.when(jnp.logical_not(need_tail))
        def _():
            for t0, _, _, pv, _ in blocks:
                store(t0, pv)

        @pl.when(need_tail)
        def _():
            for blk in blocks:
                t0, start, q2, pv, carry = blk
                store(t0, walk_tail(q2, pv, carry, start) if any(blk is t for t in tails) else pv)

    group(0, True)

    def loop_body(g, _):
        group(g * Q_GROUP, False)
        return 0

    lax.fori_loop(1, seq // Q_GROUP, loop_body, 0)


def _suffix_matrix():
    j = lax.broadcasted_iota(jnp.int32, (ATTN_WINDOW, ATTN_WINDOW), 0)
    s = lax.broadcasted_iota(jnp.int32, (ATTN_WINDOW, ATTN_WINDOW), 1)
    return jnp.where(j > s, 1.0, 0.0).astype(BF16)


def _attention(qkv, *, batch, seq):
    assert seq % Q_GROUP == 0 and seq >= ATTN_WINDOW
    qkv3 = qkv.reshape(batch, seq, 3 * D_MODEL)
    pairs = D_MODEL // LANES
    rows = HEADS_PER_BLOCK * Q_SUB
    return pl.pallas_call(
        functools.partial(_attn_kernel, seq=seq),
        out_shape=jax.ShapeDtypeStruct((batch, seq, D_MODEL), BF16),
        grid=(batch, pairs),
        in_specs=[
            pl.BlockSpec((None, seq, LANES), lambda b, p: (b, 0, p)),
            pl.BlockSpec((None, seq, LANES), lambda b, p: (b, 0, pairs + p)),
            pl.BlockSpec((None, seq, LANES), lambda b, p: (b, 0, 2 * pairs + p)),
            _resident((ATTN_WINDOW, ATTN_WINDOW)),
        ],
        out_specs=pl.BlockSpec((None, seq, LANES), lambda b, p: (b, 0, p)),
        scratch_shapes=[pltpu.VMEM((rows, LANES), F32), pltpu.VMEM((rows, LANES), F32)],
        compiler_params=_params(("parallel", "parallel")),
        name="stickbreak_attn",
    )(qkv3, qkv3, qkv3, _suffix_matrix()).reshape(batch * seq, D_MODEL)


def _pool_kernel(x_ref, mod_ref, g_ref, win_ref, wg_ref, sc_ref, wout_ref, y_ref, u_ref, *, per_batch):
    i = pl.program_id(0)
    tm = x_ref.shape[0]
    first = (i % per_batch) == 0

    @pl.when(first)
    def _():
        u_ref[0:POOL_HALO, :] = jnp.zeros((POOL_HALO, D_MODEL), F32)

    @pl.when(jnp.logical_not(first))
    def _():
        u_ref[0:POOL_HALO, :] = u_ref[tm:tm + POOL_HALO, :]

    row_parts = _row_parts(tm)
    for rows in row_parts:
        h = _norm_modulate(x_ref[rows, :], g_ref[...], mod_ref[3:4, :], mod_ref[4:5, :])
        u_ref[POOL_HALO + rows.start:POOL_HALO + rows.stop, :] = jnp.dot(
            h.astype(BF16), win_ref[...], preferred_element_type=F32)
    for rows in row_parts:
        n = rows.stop - rows.start
        pos = (i % per_batch) * tm + rows.start + lax.broadcasted_iota(jnp.int32, (n, 1), 0)
        groups = []
        for gi, w in enumerate(POOL_WINDOWS):
            lo = gi * POOL_GROUP
            wsum = u_ref[rows.start:rows.stop + POOL_HALO, lo:lo + POOL_GROUP]
            d = 1
            while d < w:
                wsum = wsum + pltpu.roll(wsum, d, 0)
                d *= 2
            inv_count = 1.0 / jnp.minimum(pos + 1, w).astype(F32)
            u = u_ref[POOL_HALO + rows.start:POOL_HALO + rows.stop, lo:lo + POOL_GROUP]
            p = (wsum[POOL_HALO:, :] * inv_count - u).astype(BF16)
            groups.append(jnp.dot(p, wg_ref[gi], preferred_element_type=F32))
        pg = jnp.concatenate(groups, axis=1) * sc_ref[...]
        m = jnp.dot(pg.astype(BF16), wout_ref[...], preferred_element_type=F32)
        y_ref[rows, :] = x_ref[rows, :] + (1.0 + mod_ref[5:6, :]) * m


def _pool(x, mod, g, w_in, w_group, scale, w_out, *, seq):
    t, d = x.shape
    tm = ROW_TILE
    per_batch = seq // tm
    assert POOL_HALO >= max(POOL_WINDOWS) - 1
    return pl.pallas_call(
        functools.partial(_pool_kernel, per_batch=per_batch),
        out_shape=jax.ShapeDtypeStruct((t, d), F32),
        grid=(t // tm,),
        in_specs=[
            pl.BlockSpec((tm, d), lambda i: (i, 0)),
            pl.BlockSpec((None, N_MOD, d), lambda i: (i // per_batch, 0, 0)),
            _resident((1, d)),
            _resident(w_in.shape),
            _resident(w_group.shape),
            _resident((1, d)),
            _resident(w_out.shape),
        ],
        out_specs=pl.BlockSpec((tm, d), lambda i: (i, 0)),
        scratch_shapes=[pltpu.VMEM((tm + POOL_HALO, d), F32)],
        compiler_params=_params(("arbitrary",)),
        name="pool_mixer",
    )(x, mod, g.reshape(1, d), w_in, w_group, scale.reshape(1, d), w_out)


def kernel(x, c, mod_w, mod_b, norm_g, ffn_w1, ffn_w2, attn_w_in, attn_w_out,
           pool_w_in, pool_w_group, pool_scale, pool_w_out, final_norm):
    batch, seq, d = x.shape
    depth = mod_w.shape[0]
    assert d == D_MODEL and seq % ROW_TILE == 0 and seq % Q_BLOCK == 0
    mod = _mod_table(c, mod_w, mod_b).reshape(depth, batch, N_MOD, d)
    xf = x.reshape(batch * seq, d)
    w1, w2 = ffn_w1.astype(BF16), ffn_w2.astype(BF16)
    for i in range(depth):
        last = i == depth - 1
        xf = _ffn(xf, mod[i], norm_g[i, 0], w1, w2, (i, 0), final_norm, seq=seq, mod_row=0, final_norm=False)
        j = i // 2
        attn = None
        if i % 2 == 0:
            qkv = _qkv(xf, mod[i], norm_g[i, 1], attn_w_in[j].astype(BF16), seq=seq)
            attn = (_attention(qkv, batch=batch, seq=seq), attn_w_out[j].astype(BF16))
        else:
            xf = _pool(xf, mod[i], norm_g[i, 1], pool_w_in[j].astype(BF16), pool_w_group[j].astype(BF16),
                       pool_scale[j], pool_w_out[j].astype(BF16), seq=seq)
        xf = _ffn(xf, mod[i], norm_g[i, 2], w1, w2, (i, 1), final_norm,
                  seq=seq, mod_row=6, final_norm=last, attn=attn)
    return xf.reshape(batch, seq, d)
```
